```python
import math
import jax, jax.numpy as jnp
from jax import lax
import numpy as np

D_MODEL = 1024
BATCH = 2
SEQ = 8192
DEPTH = 4

ATTN_HEADS_PER_GROUP = 8
ATTN_HEAD_DIM = 128
ATTN_WINDOWS = (128, 512, 2048)
ATTN_DILATIONS = (1, 4, 16)
N_ATTN_GROUPS = 3
ATTN_BLOCK = 128
ROPE_THETA = 500000.0
ROPE_DIM = ATTN_HEAD_DIM // 4
ATTN_QKV_WIDTH = N_ATTN_GROUPS * 3 * ATTN_HEADS_PER_GROUP * ATTN_HEAD_DIM
ATTN_OUT_WIDTH = ATTN_HEADS_PER_GROUP * ATTN_HEAD_DIM

SSM_EXPAND = 2
SSM_D_INNER = SSM_EXPAND * D_MODEL
SSM_HEAD_DIM = 64
SSM_HEADS = SSM_D_INNER // SSM_HEAD_DIM
SSM_STATE = 128
SSM_GROUPS = 8
SSM_CONV = 4
SSM_CHUNK = 128
SSM_CONV_DIM = SSM_D_INNER + 2 * SSM_GROUPS * SSM_STATE
SSM_IN_WIDTH = SSM_D_INNER + SSM_CONV_DIM + SSM_HEADS

D_FF = 2816
FFN_CONV = 3

NORM_EPS = 1e-5

kernel_name = "hybrid_dilated_attn_mamba2_convffn"


def rmsnorm(x, w):
    xf = x.astype(jnp.float32)
    y = xf * lax.rsqrt(jnp.mean(xf * xf, axis=-1, keepdims=True) + NORM_EPS)
    return (y * w.astype(jnp.float32)).astype(x.dtype)


def gated_group_rmsnorm(y, z, w, groups):
    g = y.astype(jnp.float32) * jax.nn.silu(z.astype(jnp.float32))
    shp = g.shape
    g = g.reshape(shp[:-1] + (groups, shp[-1] // groups))
    g = g * lax.rsqrt(jnp.mean(g * g, axis=-1, keepdims=True) + NORM_EPS)
    return g.reshape(shp) * w.astype(jnp.float32)


def causal_depthwise_conv(x, w, b):
    k = w.shape[0]
    y = lax.conv_general_dilated(
        x, w[:, None, :].astype(x.dtype), window_strides=(1,), padding=[(k - 1, 0)],
        dimension_numbers=("NWC", "WIO", "NWC"), feature_group_count=x.shape[-1])
    return y + b.astype(x.dtype)


def rope_tables(seq):
    pos = jnp.arange(seq, dtype=jnp.float32)
    inv_freq = ROPE_THETA ** (-jnp.arange(0, ROPE_DIM, 2, dtype=jnp.float32) / ROPE_DIM)
    ang = pos[:, None] * inv_freq[None, :]
    return jnp.cos(ang), jnp.sin(ang)


def apply_partial_rope(t, cos, sin):
    t = t.astype(jnp.float32)
    half = ROPE_DIM // 2
    c = cos[:, None, None, :]
    s = sin[:, None, None, :]
    x1 = t[..., :half]
    x2 = t[..., half:ROPE_DIM]
    return jnp.concatenate([x1 * c - x2 * s, x2 * c + x1 * s, t[..., ROPE_DIM:]], axis=-1)


def dilated_window_attention(q, k, v, dilation, steps):
    bsz, s, h, hd = q.shape
    length = s // dilation
    nb = -(-length // ATTN_BLOCK)
    lp = nb * ATTN_BLOCK

    def to_strided(t):
        t = t.reshape(bsz, length, dilation, h, hd).transpose(0, 2, 3, 1, 4)
        t = jnp.pad(t, ((0, 0), (0, 0), (0, 0), (0, lp - length), (0, 0)))
        return t.reshape(bsz, dilation, h, nb, ATTN_BLOCK, hd)

    def with_prev(t):
        prev = jnp.pad(t, ((0, 0), (0, 0), (0, 0), (1, 0), (0, 0), (0, 0)))[:, :, :, :-1]
        return jnp.concatenate([prev, t], axis=-2)

    qb = to_strided(q)
    kk = with_prev(to_strided(k))
    vv = with_prev(to_strided(v))
    scores = jnp.einsum("brhnqe,brhnke->brhnqk", qb, kk) * (hd ** -0.5)
    n_idx = jnp.arange(nb)[:, None, None]
    i_idx = jnp.arange(ATTN_BLOCK)[None, :, None]
    j_idx = jnp.arange(2 * ATTN_BLOCK)[None, None, :]
    delta = ATTN_BLOCK + i_idx - j_idx
    key_pos = (n_idx - 1) * ATTN_BLOCK + j_idx
    allowed = (delta >= 0) & (delta <= steps) & (key_pos >= 0)
    scores = jnp.where(allowed, scores, -jnp.inf)
    m = jnp.max(scores, axis=-1, keepdims=True)
    p = jnp.exp(scores - m)
    den = jnp.sum(p, axis=-1, keepdims=True)
    o = jnp.einsum("brhnqk,brhnke->brhnqe", p, vv) / den
    lse = (m + jnp.log(den))[..., 0]
    o = o.reshape(bsz, dilation, h, lp, hd)[:, :, :, :length]
    o = o.transpose(0, 3, 1, 2, 4).reshape(bsz, s, h, hd)
    lse = lse.reshape(bsz, dilation, h, lp)[:, :, :, :length]
    lse = lse.transpose(0, 3, 1, 2).reshape(bsz, s, h)
    return o, lse


def dilated_attention_mixer(h, w_qkv, w_o, cos, sin):
    bsz, s, _ = h.shape
    qkv = (h @ w_qkv).reshape(bsz, s, N_ATTN_GROUPS, 3, ATTN_HEADS_PER_GROUP, ATTN_HEAD_DIM)
    q = apply_partial_rope(qkv[:, :, :, 0], cos, sin)
    k = apply_partial_rope(qkv[:, :, :, 1], cos, sin)
    v = qkv[:, :, :, 2].astype(jnp.float32)
    outs, lses = [], []
    for g in range(N_ATTN_GROUPS):
        dil = ATTN_DILATIONS[g]
        o_g, lse_g = dilated_window_attention(q[:, :, g], k[:, :, g], v[:, :, g], dil,
                                              ATTN_WINDOWS[g] // dil)
        outs.append(o_g)
        lses.append(lse_g)
    wts = jax.nn.softmax(jnp.stack(lses, axis=2), axis=2)
    o = jnp.einsum("bsgh,bsghe->bshe", wts, jnp.stack(outs, axis=2))
    return o.reshape(bsz, s, ATTN_OUT_WIDTH).astype(h.dtype) @ w_o


def ssd_chunked_scan(x, dt, a, bm, cm):
    b, s, h, p = x.shape
    g, n = bm.shape[2], bm.shape[3]
    r = h // g
    c = s // SSM_CHUNK
    q = SSM_CHUNK
    x = x.reshape(b, c, q, g, r, p)
    dt = dt.reshape(b, c, q, g, r)
    bm = bm.reshape(b, c, q, g, n)
    cm = cm.reshape(b, c, q, g, n)
    a_dt = dt * a.reshape(g, r)
    a_cs = jnp.cumsum(a_dt, axis=2)
    xdt = x * dt[..., None]
    seg = a_cs[:, :, :, None] - a_cs[:, :, None, :]
    causal = jnp.tril(jnp.ones((q, q), dtype=bool))[:, :, None, None]
    lmat = jnp.exp(jnp.where(causal, seg, -jnp.inf))
    cb = jnp.einsum("bcign,bcjgn->bcijg", cm, bm)
    y_diag = jnp.einsum("bcijgr,bcjgrp->bcigrp", cb[..., None] * lmat, xdt)
    decay = jnp.exp(a_cs[:, :, -1:] - a_cs)
    states = jnp.einsum("bcjgn,bcjgr,bcjgrp->bcgrpn", bm, decay, xdt)
    chunk_decay = jnp.exp(a_cs[:, :, -1])

    def step(state, inp):
        st_c, dec_c = inp
        return state * dec_c[..., None, None] + st_c, state

    init = jnp.zeros((b, g, r, p, n), dtype=x.dtype)
    _, prev = lax.scan(step, init, (jnp.moveaxis(states, 1, 0), jnp.moveaxis(chunk_decay, 1, 0)))
    prev = jnp.moveaxis(prev, 0, 1)
    y_off = jnp.einsum("bcign,bcgrpn->bcigrp", cm, prev) * jnp.exp(a_cs)[..., None]
    return (y_diag + y_off).reshape(b, s, h, p)


def ssd_mixer(h, w_in, conv_w, conv_b, dt_bias, a_log, d_skip, norm_w, w_out):
    bsz, s, _ = h.shape
    gn = SSM_GROUPS * SSM_STATE
    zxbcdt = h @ w_in
    z = zxbcdt[..., :SSM_D_INNER]
    xbc = zxbcdt[..., SSM_D_INNER:SSM_D_INNER + SSM_CONV_DIM]
    dt_raw = zxbcdt[..., SSM_D_INNER + SSM_CONV_DIM:]
    xbc = jax.nn.silu(causal_depthwise_conv(xbc, conv_w, conv_b))
    xs = xbc[..., :SSM_D_INNER].reshape(bsz, s, SSM_HEADS, SSM_HEAD_DIM).astype(jnp.float32)
    bm = xbc[..., SSM_D_INNER:SSM_D_INNER + gn].reshape(bsz, s, SSM_GROUPS, SSM_STATE)
    cm = xbc[..., SSM_D_INNER + gn:].reshape(bsz, s, SSM_GROUPS, SSM_STATE)
    dt = jax.nn.softplus(dt_raw.astype(jnp.float32) + dt_bias.astype(jnp.float32))
    a = -jnp.exp(a_log.astype(jnp.float32))
    y = ssd_chunked_scan(xs, dt, a, bm.astype(jnp.float32), cm.astype(jnp.float32))
    y = y + d_skip.astype(jnp.float32)[:, None] * xs
    y = gated_group_rmsnorm(y.reshape(bsz, s, SSM_D_INNER), z, norm_w, SSM_GROUPS)
    return y.astype(h.dtype) @ w_out


def conv_ffn(h, w_up, conv_w, conv_b, w_down):
    u = causal_depthwise_conv(h @ w_up, conv_w, conv_b)
    gate, up = u[..., :D_FF], u[..., D_FF:]
    return (jax.nn.silu(gate) * up) @ w_down


def setup_inputs(seed: int = 0) -> dict:
    key = jax.random.key(seed)
    ks = jax.random.split(key, 20)
    n_attn = (DEPTH + 1) // 2
    n_ssm = DEPTH // 2
    f32 = jnp.float32

    def nrm(k, shape, scale):
        return jax.random.normal(k, shape, dtype=f32) * scale

    u = jax.random.uniform(ks[7], (n_ssm, SSM_HEADS), dtype=f32)
    dt0 = jnp.exp(u * (math.log(0.1) - math.log(0.001)) + math.log(0.001))
    dt0 = jnp.maximum(dt0, 1e-4)
    dt_bias = dt0 + jnp.log(-jnp.expm1(-dt0))
    a_log = jnp.log(jax.random.uniform(ks[8], (n_ssm, SSM_HEADS), dtype=f32, minval=1.0, maxval=16.0))
    return {
        "x": nrm(ks[0], (BATCH, SEQ, D_MODEL), 1.0),
        "mix_norm_w": 1.0 + nrm(ks[1], (DEPTH, D_MODEL), 0.02),
        "attn_w_qkv": nrm(ks[2], (n_attn, D_MODEL, ATTN_QKV_WIDTH), D_MODEL ** -0.5),
        "attn_w_o": nrm(ks[3], (n_attn, ATTN_OUT_WIDTH, D_MODEL), ATTN_OUT_WIDTH ** -0.5),
        "ssm_w_in": nrm(ks[4], (n_ssm, D_MODEL, SSM_IN_WIDTH), D_MODEL ** -0.5),
        "ssm_conv_w": nrm(ks[5], (n_ssm, SSM_CONV, SSM_CONV_DIM), SSM_CONV ** -0.5),
        "ssm_conv_b": nrm(ks[6], (n_ssm, SSM_CONV_DIM), 0.01),
        "ssm_dt_bias": dt_bias,
        "ssm_a_log": a_log,
        "ssm_d": 1.0 + nrm(ks[9], (n_ssm, SSM_HEADS), 0.1),
        "ssm_norm_w": 1.0 + nrm(ks[10], (n_ssm, SSM_D_INNER), 0.02),
        "ssm_w_out": nrm(ks[11], (n_ssm, SSM_D_INNER, D_MODEL), SSM_D_INNER ** -0.5),
        "ffn_norm_w": 1.0 + nrm(ks[12], (DEPTH, D_MODEL), 0.02),
        "ffn_w_up": nrm(ks[13], (DEPTH, D_MODEL, 2 * D_FF), D_MODEL ** -0.5),
        "ffn_conv_w": nrm(ks[14], (DEPTH, FFN_CONV, 2 * D_FF), FFN_CONV ** -0.5),
        "ffn_conv_b": nrm(ks[15], (DEPTH, 2 * D_FF), 0.01),
        "ffn_w_down": nrm(ks[16], (DEPTH, D_FF, D_MODEL), D_FF ** -0.5),
        "final_norm_w": 1.0 + nrm(ks[17], (D_MODEL,), 0.02),
    }


def reference(x, mix_norm_w, attn_w_qkv, attn_w_o, ssm_w_in, ssm_conv_w, ssm_conv_b,
              ssm_dt_bias, ssm_a_log, ssm_d, ssm_norm_w, ssm_w_out, ffn_norm_w, ffn_w_up,
              ffn_conv_w, ffn_conv_b, ffn_w_down, final_norm_w):
    cos, sin = rope_tables(x.shape[1])
    for i in range(DEPTH):
        h = rmsnorm(x, mix_norm_w[i])
        j = i // 2
        if i % 2 == 0:
            x = x + dilated_attention_mixer(h, attn_w_qkv[j], attn_w_o[j], cos, sin)
        else:
            x = x + ssd_mixer(h, ssm_w_in[j], ssm_conv_w[j], ssm_conv_b[j], ssm_dt_bias[j],
                              ssm_a_log[j], ssm_d[j], ssm_norm_w[j], ssm_w_out[j])
        h = rmsnorm(x, ffn_norm_w[i])
        x = x + conv_ffn(h, ffn_w_up[i], ffn_conv_w[i], ffn_conv_b[i], ffn_w_down[i])
    return rmsnorm(x, final_norm_w)
```

```python
import functools
import math

import jax
import jax.numpy as jnp
from jax import lax
from jax.experimental import pallas as pl
from jax.experimental.pallas import tpu as pltpu

F32 = jnp.float32
BF16 = jnp.bfloat16

NORM_EPS = 1e-5
ROPE_THETA = 500000.0

HEAD_DIM = 128
ATTN_HEADS = 8
ATTN_WINDOWS = (128, 512, 2048)
ATTN_DILATIONS = (1, 4, 16)
ATTN_STEPS = 128
ROPE_DIM = HEAD_DIM // 4

SSM_HEAD_DIM = 64
SSM_HEADS = 32
SSM_STATE = 128
SSM_GROUPS = 8
SSM_CHUNK = 128
SSM_D_INNER = SSM_HEADS * SSM_HEAD_DIM
SSM_GROUP_WIDTH = SSM_D_INNER // SSM_GROUPS
SSM_HEADS_PER_GROUP = SSM_HEADS // SSM_GROUPS

LANES = 128
VMEM_LIMIT_BYTES = 56 * 1024 * 1024

ROW_TILE = 512
FFN_CHUNK = 256
PROJ_CHUNK = 512


def _cparams(semantics):
    return pltpu.CompilerParams(dimension_semantics=semantics, vmem_limit_bytes=VMEM_LIMIT_BYTES)


def _resident(shape):
    nd = len(shape)
    return pl.BlockSpec(shape, lambda *_: (0,) * nd, pipeline_mode=pl.Buffered(1))


def _rms(x, w):
    ms = jnp.mean(x * x, axis=-1, keepdims=True)
    return x * lax.rsqrt(ms + NORM_EPS) * w


def _silu(x):
    return x / (1.0 + jnp.exp(-x))


def _shift_rows(u, k, tail):
    rolled = pltpu.roll(u, k, 0)
    row = lax.broadcasted_iota(jnp.int32, (8, u.shape[1]), 0)
    top = rolled[0:8]
    for j in range(k):
        top = jnp.where(row == j, tail[8 - k + j:8 - k + j + 1], top)
    return jnp.concatenate([top, rolled[8:]], axis=0)


def _qkv_body(x_ref, nw_ref, w_ref, cos_ref, sa_ref, sb_ref, o_ref):
    hb = _rms(x_ref[0], nw_ref[...]).astype(BF16)
    cos = cos_ref[...]
    sa = sa_ref[...]
    sb = sb_ref[...]
    width = w_ref.shape[1]
    hw = ATTN_HEADS * HEAD_DIM
    scale = HEAD_DIM ** -0.5
    for c in range(width // PROJ_CHUNK):
        c0 = c * PROJ_CHUNK
        y = jnp.dot(hb, w_ref[:, c0:c0 + PROJ_CHUNK], preferred_element_type=F32)
        if c0 < 2 * hw:
            parts = []
            for j in range(PROJ_CHUNK // HEAD_DIM):
                t = y[:, j * HEAD_DIM:(j + 1) * HEAD_DIM]
                t = (t * cos + pltpu.roll(t, HEAD_DIM - ROPE_DIM // 2, 1) * sa
                     + pltpu.roll(t, ROPE_DIM // 2, 1) * sb)
                if c0 < hw:
                    t = t * scale
                parts.append(t)
            y = jnp.concatenate(parts, axis=1)
        o_ref[0, 0, :, c0:c0 + PROJ_CHUNK] = y.astype(BF16)


def _qkv_proj(x, norm_w, w_g, cos_t, sa_t, sb_t, dil):
    bsz, s, d = x.shape
    length = s // dil
    tl = min(ROW_TILE, length)
    width = w_g.shape[1]
    x3 = x.reshape(bsz, length, dil * d)
    tabs = [t.reshape(length, dil * LANES) for t in (cos_t, sa_t, sb_t)]
    tab_spec = pl.BlockSpec((tl, LANES), lambda b, r, l: (l, r))
    return pl.pallas_call(
        _qkv_body,
        out_shape=jax.ShapeDtypeStruct((bsz, dil, length, width), BF16),
        grid=(bsz, dil, length // tl),
        in_specs=[
            pl.BlockSpec((1, tl, d), lambda b, r, l: (b, l, r)),
            _resident((1, d)),
            _resident((d, width)),
            tab_spec, tab_spec, tab_spec,
        ],
        out_specs=pl.BlockSpec((1, 1, tl, width), lambda b, r, l: (b, r, l, 0)),
        compiler_params=_cparams(("parallel", "parallel", "parallel")),
        name=f"qkv_proj_d{dil}",
    )(x3, norm_w.reshape(1, d), w_g, *tabs)


def _attn_body(q_ref, kc_ref, vc_ref, kp_ref, vp_ref, o_ref, lse_ref):
    tq = q_ref.shape[2]
    blk = ATTN_STEPS
    first = pl.program_id(2) == 0
    row = lax.broadcasted_iota(jnp.int32, (blk, 2 * blk), 0)
    col = lax.broadcasted_iota(jnp.int32, (blk, 2 * blk), 1)
    band = (col >= row) & (col <= row + blk)
    lane = lax.broadcasted_iota(jnp.int32, (blk, LANES), 1)
    for jb in range(tq // blk):
        r0 = jb * blk
        if jb == 0:
            mask = band & (col >= jnp.where(first, blk, 0))
        else:
            mask = band
        lse_all = jnp.zeros((blk, LANES), F32)
        for h in range(ATTN_HEADS):
            cs = slice(h * HEAD_DIM, (h + 1) * HEAD_DIM)
            q = q_ref[0, 0, r0:r0 + blk, cs]
            if jb == 0:
                k = jnp.concatenate([kp_ref[0, 0, :, cs], kc_ref[0, 0, 0:blk, cs]], axis=0)
                v = jnp.concatenate([vp_ref[0, 0, :, cs], vc_ref[0, 0, 0:blk, cs]], axis=0)
            else:
                k = kc_ref[0, 0, r0 - blk:r0 + blk, cs]
                v = vc_ref[0, 0, r0 - blk:r0 + blk, cs]
            s = lax.dot_general(q, k, (((1,), (1,)), ((), ())), preferred_element_type=F32)
            s = jnp.where(mask, s, -jnp.inf)
            m = jnp.max(s, axis=-1, keepdims=True)
            p = jnp.exp(s - m)
            den = jnp.sum(p, axis=-1, keepdims=True)
            o = jnp.dot(p.astype(BF16), v, preferred_element_type=F32) / den
            o_ref[0, r0:r0 + blk, cs] = o.astype(BF16)
            lse_all = jnp.where(lane == h, m + jnp.log(den), lse_all)
        lse_ref[0, r0:r0 + blk, :] = lse_all


def _window_attention(qkv, dil):
    bsz, _, length, width = qkv.shape
    hw = width // 3
    tq = min(ROW_TILE, length)
    nprev = tq // ATTN_STEPS
    cur = lambda col: pl.BlockSpec((1, 1, tq, hw), lambda b, r, i: (b, r, i, col))
    prev = lambda col: pl.BlockSpec(
        (1, 1, ATTN_STEPS, hw), lambda b, r, i: (b, r, jnp.maximum(i * nprev - 1, 0), col))
    o, lse = pl.pallas_call(
        _attn_body,
        out_shape=(jax.ShapeDtypeStruct((bsz, length, dil * hw), BF16),
                   jax.ShapeDtypeStruct((bsz, length, dil * LANES), F32)),
        grid=(bsz, dil, length // tq),
        in_specs=[cur(0), cur(1), cur(2), prev(1), prev(2)],
        out_specs=(pl.BlockSpec((1, tq, hw), lambda b, r, i: (b, i, r)),
                   pl.BlockSpec((1, tq, LANES), lambda b, r, i: (b, i, r))),
        compiler_params=_cparams(("parallel", "parallel", "parallel")),
        name=f"window_attn_d{dil}",
    )(qkv, qkv, qkv, qkv, qkv)
    return o.reshape(bsz, length * dil, hw), lse.reshape(bsz, length * dil, LANES)


def _attn_out_body(x_ref, o0_ref, o1_ref, o2_ref, l0_ref, l1_ref, l2_ref, w_ref, out_ref):
    o_refs = (o0_ref, o1_ref, o2_ref)
    ls = [l0_ref[...], l1_ref[...], l2_ref[...]]
    m = jnp.maximum(jnp.maximum(ls[0], ls[1]), ls[2])
    es = [jnp.exp(l - m) for l in ls]
    inv = 1.0 / (es[0] + es[1] + es[2])
    acc = None
    for g in range(3):
        wg = es[g] * inv
        og = o_refs[g][...].astype(F32)
        parts = [og[:, h * HEAD_DIM:(h + 1) * HEAD_DIM] * wg[:, h:h + 1] for h in range(ATTN_HEADS)]
        og = jnp.concatenate(parts, axis=1)
        acc = og if acc is None else acc + og
    y = jnp.dot(acc.astype(BF16), w_ref[...], preferred_element_type=F32)
    out_ref[...] = x_ref[...] + y


def _attn_out(x2, os, lses, w_o):
    t, d = x2.shape
    tm = min(ROW_TILE, t)
    hw = w_o.shape[0]
    row = lambda w: pl.BlockSpec((tm, w), lambda i: (i, 0))
    return pl.pallas_call(
        _attn_out_body,
        out_shape=jax.ShapeDtypeStruct((t, d), F32),
        grid=(t // tm,),
        in_specs=[row(d), row(hw), row(hw), row(hw), row(LANES), row(LANES), row(LANES),
                  _resident((hw, d))],
        out_specs=row(d),
        compiler_params=_cparams(("parallel",)),
        name="attn_out",
    )(x2, *os, *lses, w_o)


def _proj_res_body(x_ref, y_ref, w_ref, out_ref):
    out_ref[...] = x_ref[...] + jnp.dot(y_ref[...], w_ref[...], preferred_element_type=F32)


def _proj_res(x2, y, w):
    t, d = x2.shape
    tm = min(ROW_TILE, t)
    k = w.shape[0]
    return pl.pallas_call(
        _proj_res_body,
        out_shape=jax.ShapeDtypeStruct((t, d), F32),
        grid=(t // tm,),
        in_specs=[pl.BlockSpec((tm, d), lambda i: (i, 0)), pl.BlockSpec((tm, k), lambda i: (i, 0)),
                  _resident((k, d))],
        out_specs=pl.BlockSpec((tm, d), lambda i: (i, 0)),
        compiler_params=_cparams(("parallel",)),
        name="proj_res",
    )(x2, y, w)


def _ffn_body(x_ref, nw_ref, wg_ref, wu_ref, cwg_ref, cwu_ref, wd_ref, fw_ref, out_ref,
              tail_g, tail_u, acc_ref, *, final_norm):
    n_chunks = wg_ref.shape[0]
    x = x_ref[0]
    hb = _rms(x, nw_ref[...]).astype(BF16)
    tm = x.shape[0]

    @pl.when(pl.program_id(1) == 0)
    def _():
        tail_g[...] = jnp.zeros_like(tail_g)
        tail_u[...] = jnp.zeros_like(tail_u)

    acc_ref[...] = jnp.zeros_like(acc_ref)

    def conv(u, tail, cw):
        return (_shift_rows(u, 2, tail) * cw[0:1] + _shift_rows(u, 1, tail) * cw[1:2]
                + u * cw[2:3] + cw[3:4])

    def chunk(c, carry):
        ug = jnp.dot(hb, wg_ref[c], preferred_element_type=F32)
        uu = jnp.dot(hb, wu_ref[c], preferred_element_type=F32)
        g = conv(ug, tail_g[c], cwg_ref[c])
        u = conv(uu, tail_u[c], cwu_ref[c])
        tail_g[c] = ug[tm - 8:tm]
        tail_u[c] = uu[tm - 8:tm]
        act = (_silu(g) * u).astype(BF16)
        acc_ref[...] += jnp.dot(act, wd_ref[c], preferred_element_type=F32)
        return carry

    lax.fori_loop(0, n_chunks, chunk, 0)
    y = x + acc_ref[...]
    if final_norm:
        y = _rms(y, fw_ref[...])
    out_ref[0] = y


def _conv_ffn(x, norm_w, w_up, conv_w, conv_b, w_down, final_w):
    bsz, s, d = x.shape
    d_ff = w_down.shape[0]
    cw = FFN_CHUNK
    nc = d_ff // cw
    tm = min(ROW_TILE, s)

    def chunked_cols(w):
        return w.reshape(d, nc, cw).transpose(1, 0, 2).astype(BF16)

    def conv_params(w, b):
        p = jnp.concatenate([w, b[None, :], jnp.zeros((4, d_ff), F32)], axis=0)
        return p.reshape(8, nc, cw).transpose(1, 0, 2)

    wg = chunked_cols(w_up[:, :d_ff])
    wu = chunked_cols(w_up[:, d_ff:])
    cwg = conv_params(conv_w[:, :d_ff], conv_b[:d_ff])
    cwu = conv_params(conv_w[:, d_ff:], conv_b[d_ff:])
    wd = w_down.reshape(nc, cw, d).astype(BF16)
    final_norm = final_w is not None
    fw = (final_w if final_norm else jnp.ones((d,), F32)).reshape(1, d)
    return pl.pallas_call(
        functools.partial(_ffn_body, final_norm=final_norm),
        out_shape=jax.ShapeDtypeStruct((bsz, s, d), F32),
        grid=(bsz, s // tm),
        in_specs=[
            pl.BlockSpec((1, tm, d), lambda b, i: (b, i, 0)),
            _resident((1, d)),
            _resident((nc, d, cw)), _resident((nc, d, cw)),
            _resident((nc, 8, cw)), _resident((nc, 8, cw)),
            _resident((nc, cw, d)),
            _resident((1, d)),
        ],
        out_specs=pl.BlockSpec((1, tm, d), lambda b, i: (b, i, 0)),
        scratch_shapes=[pltpu.VMEM((nc, 8, cw), F32), pltpu.VMEM((nc, 8, cw), F32),
                        pltpu.VMEM((tm, d), F32)],
        compiler_params=_cparams(("parallel", "arbitrary")),
        name="conv_ffn",
    )(x, norm_w.reshape(1, d), wg, wu, cwg, cwu, wd, fw)


def _ssm_in_body(x_ref, nw_ref, wz_ref, wx_ref, wdt_ref, cw_ref, dtb_ref,
                 z_ref, xbc_ref, dt_ref, tail):
    x = x_ref[0]
    hb = _rms(x, nw_ref[...]).astype(BF16)
    tm = x.shape[0]

    @pl.when(pl.program_id(1) == 0)
    def _():
        tail[...] = jnp.zeros_like(tail)

    for c in range(wz_ref.shape[0]):
        z = jnp.dot(hb, wz_ref[c], preferred_element_type=F32)
        z_ref[0, :, c * PROJ_CHUNK:(c + 1) * PROJ_CHUNK] = z.astype(BF16)

    for c in range(wx_ref.shape[0]):
        u = jnp.dot(hb, wx_ref[c], preferred_element_type=F32)
        t8 = tail[c]
        cw = cw_ref[c]
        y = (_shift_rows(u, 3, t8) * cw[0:1] + _shift_rows(u, 2, t8) * cw[1:2]
             + _shift_rows(u, 1, t8) * cw[2:3] + u * cw[3:4] + cw[4:5])
        tail[c] = u[tm - 8:tm]
        xbc_ref[0, :, c * PROJ_CHUNK:(c + 1) * PROJ_CHUNK] = _silu(y).astype(BF16)

    v = jnp.dot(hb, wdt_ref[...], preferred_element_type=F32) + dtb_ref[...]
    dt_ref[0] = jnp.maximum(v, 0.0) + jnp.log(1.0 + jnp.exp(-jnp.abs(v)))


def _ssm_in(x, norm_w, w_in, conv_w, conv_b, dt_bias):
    bsz, s, d = x.shape
    conv_dim = conv_w.shape[1]
    tm = min(ROW_TILE, s)
    cw = PROJ_CHUNK
    nz = SSM_D_INNER // cw
    nx = conv_dim // cw

    def chunked_cols(w, n):
        return w.reshape(d, n, cw).transpose(1, 0, 2).astype(BF16)

    wz = chunked_cols(w_in[:, :SSM_D_INNER], nz)
    wx = chunked_cols(w_in[:, SSM_D_INNER:SSM_D_INNER + conv_dim], nx)
    wdt = jnp.pad(w_in[:, SSM_D_INNER + conv_dim:], ((0, 0), (0, LANES - SSM_HEADS))).astype(BF16)
    cp = jnp.concatenate([conv_w, conv_b[None, :], jnp.zeros((3, conv_dim), F32)], axis=0)
    cp = cp.reshape(8, nx, cw).transpose(1, 0, 2)
    dtb = jnp.pad(dt_bias, (0, LANES - SSM_HEADS)).reshape(1, LANES)
    row = lambda w: pl.BlockSpec((1, tm, w), lambda b, i: (b, i, 0))
    return pl.pallas_call(
        _ssm_in_body,
        out_shape=(jax.ShapeDtypeStruct((bsz, s, SSM_D_INNER), BF16),
                   jax.ShapeDtypeStruct((bsz, s, conv_dim), BF16),
                   jax.ShapeDtypeStruct((bsz, s, LANES), F32)),
        grid=(bsz, s // tm),
        in_specs=[row(d), _resident((1, d)), _resident((nz, d, cw)), _resident((nx, d, cw)),
                  _resident((d, LANES)), _resident((nx, 8, cw)), _resident((1, LANES))],
        out_specs=(row(SSM_D_INNER), row(conv_dim), row(LANES)),
        scratch_shapes=[pltpu.VMEM((nx, 8, cw), F32)],
        compiler_params=_cparams(("parallel", "arbitrary")),
        name="ssm_in",
    )(x, norm_w.reshape(1, d), wz, wx, wdt, cp, dtb)


def _ssd_body(xs_ref, b_ref, c_ref, dt_ref, z_ref, a_ref, e_ref, dsk_ref, nw_ref, y_ref, state):
    q = SSM_CHUNK
    gw = SSM_GROUP_WIDTH

    @pl.when(pl.program_id(1) == 0)
    def _():
        state[...] = jnp.zeros_like(state)

    dt = dt_ref[0]
    row = lax.broadcasted_iota(jnp.int32, (q, LANES), 0)
    lane = lax.broadcasted_iota(jnp.int32, (q, LANES), 1)
    acs = dt * a_ref[...]
    k = 1
    while k < q:
        acs = acs + jnp.where(row >= k, pltpu.roll(acs, k, 0), 0.0)
        k *= 2
    acs_t = acs.T
    eacs = jnp.exp(acs)
    ddec = dt * jnp.exp(acs[q - 1:q] - acs)

    nh = SSM_HEADS
    packed = jnp.where(lane < nh, dt,
                       jnp.where(lane < 2 * nh, pltpu.roll(eacs, nh, 1),
                                 jnp.where(lane < 3 * nh, pltpu.roll(ddec, 2 * nh, 1), 0.0)))
    hi = packed.astype(BF16)
    lo = (packed - hi.astype(F32)).astype(BF16)
    expanded = jnp.dot(jnp.concatenate([hi, lo], axis=1), e_ref[...], preferred_element_type=F32)
    di = SSM_D_INNER
    dt_x = expanded[:, 0:di]
    eacs_x = expanded[:, di:2 * di]
    ddec_x = expanded[:, 2 * di:3 * di]

    causal = (lax.broadcasted_iota(jnp.int32, (q, q), 0) >= lax.broadcasted_iota(jnp.int32, (q, q), 1))
    for g in range(SSM_GROUPS):
        gs = slice(g * gw, (g + 1) * gw)
        ns = slice(g * SSM_STATE, (g + 1) * SSM_STATE)
        xg = xs_ref[0, :, gs].astype(F32)
        bg = b_ref[0, :, ns]
        cg = c_ref[0, :, ns]
        xdt = (xg * dt_x[:, gs]).astype(BF16)
        cb = lax.dot_general(cg, bg, (((1,), (1,)), ((), ())), preferred_element_type=F32)
        s_prev = state[g]
        y_off = jnp.dot(cg, s_prev.astype(BF16), preferred_element_type=F32) * eacs_x[:, gs]
        ys = []
        for r in range(SSM_HEADS_PER_GROUP):
            h = g * SSM_HEADS_PER_GROUP + r
            seg = acs[:, h:h + 1] - acs_t[h:h + 1, :]
            lmat = jnp.exp(jnp.where(causal, seg, -jnp.inf))
            mh = (cb * lmat).astype(BF16)
            ys.append(jnp.dot(mh, xdt[:, r * SSM_HEAD_DIM:(r + 1) * SSM_HEAD_DIM],
                              preferred_element_type=F32))
        y = jnp.concatenate(ys, axis=1) + y_off
        xdd = (xg * ddec_x[:, gs]).astype(BF16)
        upd = lax.dot_general(bg, xdd, (((0,), (0,)), ((), ())), preferred_element_type=F32)
        state[g] = s_prev * eacs_x[q - 1:q, gs] + upd

        y = y + dsk_ref[:, gs] * xg
        gz = y * _silu(z_ref[0, :, gs].astype(F32))
        gz = gz * lax.rsqrt(jnp.mean(gz * gz, axis=-1, keepdims=True) + NORM_EPS)
        y_ref[0, :, gs] = (gz * nw_ref[:, gs]).astype(BF16)


def _ssd_scan(xbc, dt, z, a_log, d_skip, norm_w):
    bsz, s, conv_dim = xbc.shape
    q = SSM_CHUNK
    di = SSM_D_INNER
    gn = SSM_GROUPS * SSM_STATE
    a = jnp.pad(-jnp.exp(a_log), (0, LANES - SSM_HEADS)).reshape(1, LANES)
    src = jnp.arange(LANES)
    dst = jnp.arange(3 * di)
    e1 = ((src[:, None] // SSM_HEADS == dst[None, :] // di)
          & (src[:, None] % SSM_HEADS == (dst[None, :] % di) // SSM_HEAD_DIM)
          & (src[:, None] < 3 * SSM_HEADS))
    e2 = jnp.concatenate([e1, e1], axis=0).astype(BF16)
    dsk = jnp.repeat(d_skip, SSM_HEAD_DIM).reshape(1, di)
    assert gn * 2 + di == conv_dim
    return pl.pallas_call(
        _ssd_body,
        out_shape=jax.ShapeDtypeStruct((bsz, s, di), BF16),
        grid=(bsz, s // q),
        in_specs=[
            pl.BlockSpec((1, q, di), lambda b, c: (b, c, 0)),
            pl.BlockSpec((1, q, gn), lambda b, c: (b, c, di // gn)),
            pl.BlockSpec((1, q, gn), lambda b, c: (b, c, di // gn + 1)),
            pl.BlockSpec((1, q, LANES), lambda b, c: (b, c, 0)),
            pl.BlockSpec((1, q, di), lambda b, c: (b, c, 0)),
            _resident((1, LANES)),
            _resident((2 * LANES, 3 * di)),
            _resident((1, di)),
            _resident((1, di)),
        ],
        out_specs=pl.BlockSpec((1, q, di), lambda b, c: (b, c, 0)),
        scratch_shapes=[pltpu.VMEM((SSM_GROUPS, SSM_STATE, SSM_GROUP_WIDTH), F32)],
        compiler_params=_cparams(("parallel", "arbitrary")),
        name="ssd_scan",
    )(xbc, xbc, xbc, dt, z, a, e2, dsk, norm_w.reshape(1, di))


def _rope_tables(seq):
    half = ROPE_DIM // 2
    pos = jnp.arange(seq, dtype=F32)
    inv_freq = ROPE_THETA ** (-jnp.arange(0, ROPE_DIM, 2, dtype=F32) / ROPE_DIM)
    ang = pos[:, None] * inv_freq[None, :]
    cos, sin = jnp.cos(ang), jnp.sin(ang)
    ones = jnp.ones((seq, HEAD_DIM - ROPE_DIM), F32)
    zeros = jnp.zeros((seq, HEAD_DIM - ROPE_DIM), F32)
    zh = jnp.zeros((seq, half), F32)
    cos_t = jnp.concatenate([cos, cos, ones], axis=1)
    sa_t = jnp.concatenate([-sin, zh, zeros], axis=1)
    sb_t = jnp.concatenate([zh, sin, zeros], axis=1)
    return cos_t, sa_t, sb_t


def _attention_layer(x, norm_w, w_qkv, w_o, tables):
    bsz, s, d = x.shape
    gwidth = 3 * ATTN_HEADS * HEAD_DIM
    os, lses = [], []
    for g, dil in enumerate(ATTN_DILATIONS):
        w_g = w_qkv[:, g * gwidth:(g + 1) * gwidth].astype(BF16)
        qkv = _qkv_proj(x, norm_w, w_g, *tables, dil)
        o, lse = _window_attention(qkv, dil)
        os.append(o.reshape(bsz * s, -1))
        lses.append(lse.reshape(bsz * s, LANES))
    out = _attn_out(x.reshape(bsz * s, d), os, lses, w_o.astype(BF16))
    return out.reshape(bsz, s, d)


def _ssd_layer(x, norm_w, w_in, conv_w, conv_b, dt_bias, a_log, d_skip, gnorm_w, w_out):
    bsz, s, d = x.shape
    z, xbc, dt = _ssm_in(x, norm_w, w_in, conv_w, conv_b, dt_bias)
    y = _ssd_scan(xbc, dt, z, a_log, d_skip, gnorm_w)
    out = _proj_res(x.reshape(bsz * s, d), y.reshape(bsz * s, -1), w_out.astype(BF16))
    return out.reshape(bsz, s, d)


def kernel(x, mix_norm_w, attn_w_qkv, attn_w_o, ssm_w_in, ssm_conv_w, ssm_conv_b, ssm_dt_bias,
           ssm_a_log, ssm_d, ssm_norm_w, ssm_w_out, ffn_norm_w, ffn_w_up, ffn_conv_w, ffn_conv_b,
           ffn_w_down, final_norm_w):
    depth = mix_norm_w.shape[0]
    tables = _rope_tables(x.shape[1])
    for i in range(depth):
        j = i // 2
        if i % 2 == 0:
            x = _attention_layer(x, mix_norm_w[i], attn_w_qkv[j], attn_w_o[j], tables)
        else:
            x = _ssd_layer(x, mix_norm_w[i], ssm_w_in[j], ssm_conv_w[j], ssm_conv_b[j],
                           ssm_dt_bias[j], ssm_a_log[j], ssm_d[j], ssm_norm_w[j], ssm_w_out[j])
        x = _conv_ffn(x, ffn_norm_w[i], ffn_w_up[i], ffn_conv_w[i], ffn_conv_b[i], ffn_w_down[i],
                      final_norm_w if i == depth - 1 else None)
    return x
```

```python
import functools
import math

import jax
import jax.numpy as jnp
from jax import lax
from jax.experimental import pallas as pl
from jax.experimental.pallas import tpu as pltpu

F32 = jnp.float32
BF16 = jnp.bfloat16

NORM_EPS = 1e-5
ROPE_THETA = 500000.0

HEAD_DIM = 128
ATTN_HEADS = 8
ATTN_WINDOWS = (128, 512, 2048)
ATTN_DILATIONS = (1, 4, 16)
ATTN_STEPS = 128
ROPE_DIM = HEAD_DIM // 4

SSM_HEAD_DIM = 64
SSM_HEADS = 32
SSM_STATE = 128
SSM_GROUPS = 8
SSM_CHUNK = 128
SSM_D_INNER = SSM_HEADS * SSM_HEAD_DIM
SSM_GROUP_WIDTH = SSM_D_INNER // SSM_GROUPS
SSM_HEADS_PER_GROUP = SSM_HEADS // SSM_GROUPS

LANES = 128
VMEM_LIMIT_BYTES = 56 * 1024 * 1024

ROW_TILE = 512
FFN_CHUNK = 256
PROJ_CHUNK = 512


def _cparams(semantics):
    return pltpu.CompilerParams(dimension_semantics=semantics, vmem_limit_bytes=VMEM_LIMIT_BYTES)


def _resident(shape):
    nd = len(shape)
    return pl.BlockSpec(shape, lambda *_: (0,) * nd, pipeline_mode=pl.Buffered(1))


def _rms(x, w):
    ms = jnp.mean(x * x, axis=-1, keepdims=True)
    return x * lax.rsqrt(ms + NORM_EPS) * w


def _silu(x):
    return x / (1.0 + jnp.exp(-x))


def _conv_taps(u, ub, tail8, cw, taps):
    rows = u.shape[0]
    ub[0:8] = tail8
    ub[8:rows + 8] = u
    y = u * cw[taps - 1:taps] + cw[taps:taps + 1]
    for k in range(taps - 1):
        back = taps - 1 - k
        y = y + ub[8 - back:8 - back + rows] * cw[k:k + 1]
    return y


def _qkv_body(x_ref, nw_ref, w_ref, cos_ref, sa_ref, sb_ref, o_ref):
    hb = _rms(x_ref[0], nw_ref[...]).astype(BF16)
    cos = cos_ref[...]
    sa = sa_ref[...]
    sb = sb_ref[...]
    width = w_ref.shape[1]
    hw = ATTN_HEADS * HEAD_DIM
    scale = HEAD_DIM ** -0.5
    for c in range(width // PROJ_CHUNK):
        c0 = c * PROJ_CHUNK
        y = jnp.dot(hb, w_ref[:, c0:c0 + PROJ_CHUNK], preferred_element_type=F32)
        if c0 < 2 * hw:
            parts = []
            for j in range(PROJ_CHUNK // HEAD_DIM):
                t = y[:, j * HEAD_DIM:(j + 1) * HEAD_DIM]
                t = (t * cos + pltpu.roll(t, HEAD_DIM - ROPE_DIM // 2, 1) * sa
                     + pltpu.roll(t, ROPE_DIM // 2, 1) * sb)
                if c0 < hw:
                    t = t * scale
                parts.append(t)
            y = jnp.concatenate(parts, axis=1)
        o_ref[0, 0, :, c0:c0 + PROJ_CHUNK] = y.astype(BF16)


def _qkv_proj(x, norm_w, w_g, cos_t, sa_t, sb_t, dil):
    bsz, s, d = x.shape
    length = s // dil
    tl = min(ROW_TILE, length)
    width = w_g.shape[1]
    x3 = x.reshape(bsz, length, dil * d)
    tabs = [t.reshape(length, dil * LANES) for t in (cos_t, sa_t, sb_t)]
    tab_spec = pl.BlockSpec((tl, LANES), lambda b, r, l: (l, r))
    return pl.pallas_call(
        _qkv_body,
        out_shape=jax.ShapeDtypeStruct((bsz, dil, length, width), BF16),
        grid=(bsz, dil, length // tl),
        in_specs=[
            pl.BlockSpec((1, tl, d), lambda b, r, l: (b, l, r)),
            _resident((1, d)),
            _resident((d, width)),
            tab_spec, tab_spec, tab_spec,
        ],
        out_specs=pl.BlockSpec((1, 1, tl, width), lambda b, r, l: (b, r, l, 0)),
        compiler_params=_cparams(("parallel", "parallel", "parallel")),
        name=f"qkv_proj_d{dil}",
    )(x3, norm_w.reshape(1, d), w_g, *tabs)


def _attn_body(q_ref, kc_ref, vc_ref, kp_ref, vp_ref, o_ref, lse_ref):
    tq = q_ref.shape[2]
    blk = ATTN_STEPS
    first = pl.program_id(2) == 0
    row = lax.broadcasted_iota(jnp.int32, (blk, 2 * blk), 0)
    col = lax.broadcasted_iota(jnp.int32, (blk, 2 * blk), 1)
    band = (col >= row) & (col <= row + blk)
    lane = lax.broadcasted_iota(jnp.int32, (blk, LANES), 1)
    for jb in range(tq // blk):
        r0 = jb * blk
        if jb == 0:
            mask = band & (col >= jnp.where(first, blk, 0))
        else:
            mask = band
        lse_all = jnp.zeros((blk, LANES), F32)
        for h in range(ATTN_HEADS):
            cs = slice(h * HEAD_DIM, (h + 1) * HEAD_DIM)
            q = q_ref[0, 0, r0:r0 + blk, cs]
            if jb == 0:
                k = jnp.concatenate([kp_ref[0, 0, :, cs], kc_ref[0, 0, 0:blk, cs]], axis=0)
                v = jnp.concatenate([vp_ref[0, 0, :, cs], vc_ref[0, 0, 0:blk, cs]], axis=0)
            else:
                k = kc_ref[0, 0, r0 - blk:r0 + blk, cs]
                v = vc_ref[0, 0, r0 - blk:r0 + blk, cs]
            s = lax.dot_general(q, k, (((1,), (1,)), ((), ())), preferred_element_type=F32)
            s = jnp.where(mask, s, -jnp.inf)
            m = jnp.max(s, axis=-1, keepdims=True)
            p = jnp.exp(s - m)
            den = jnp.sum(p, axis=-1, keepdims=True)
            o = jnp.dot(p.astype(BF16), v, preferred_element_type=F32) / den
            o_ref[0, r0:r0 + blk, cs] = o.astype(BF16)
            lse_all = jnp.where(lane == h, m + jnp.log(den), lse_all)
        lse_ref[0, r0:r0 + blk, :] = lse_all


def _window_attention(qkv, dil):
    bsz, _, length, width = qkv.shape
    hw = width // 3
    tq = min(ROW_TILE, length)
    nprev = tq // ATTN_STEPS
    cur = lambda col: pl.BlockSpec((1, 1, tq, hw), lambda b, r, i: (b, r, i, col))
    prev = lambda col: pl.BlockSpec(
        (1, 1, ATTN_STEPS, hw), lambda b, r, i: (b, r, jnp.maximum(i * nprev - 1, 0), col))
    o, lse = pl.pallas_call(
        _attn_body,
        out_shape=(jax.ShapeDtypeStruct((bsz, length, dil * hw), BF16),
                   jax.ShapeDtypeStruct((bsz, length, dil * LANES), F32)),
        grid=(bsz, dil, length // tq),
        in_specs=[cur(0), cur(1), cur(2), prev(1), prev(2)],
        out_specs=(pl.BlockSpec((1, tq, hw), lambda b, r, i: (b, i, r)),
                   pl.BlockSpec((1, tq, LANES), lambda b, r, i: (b, i, r))),
        compiler_params=_cparams(("parallel", "parallel", "parallel")),
        name=f"window_attn_d{dil}",
    )(qkv, qkv, qkv, qkv, qkv)
    return o.reshape(bsz, length * dil, hw), lse.reshape(bsz, length * dil, LANES)


def _attn_out_body(x_ref, o0_ref, o1_ref, o2_ref, l0_ref, l1_ref, l2_ref, w_ref, out_ref):
    o_refs = (o0_ref, o1_ref, o2_ref)
    ls = [l0_ref[...], l1_ref[...], l2_ref[...]]
    m = jnp.maximum(jnp.maximum(ls[0], ls[1]), ls[2])
    es = [jnp.exp(l - m) for l in ls]
    inv = 1.0 / (es[0] + es[1] + es[2])
    acc = None
    for g in range(3):
        wg = es[g] * inv
        og = o_refs[g][...].astype(F32)
        parts = [og[:, h * HEAD_DIM:(h + 1) * HEAD_DIM] * wg[:, h:h + 1] for h in range(ATTN_HEADS)]
        og = jnp.concatenate(parts, axis=1)
        acc = og if acc is None else acc + og
    y = jnp.dot(acc.astype(BF16), w_ref[...], preferred_element_type=F32)
    out_ref[...] = x_ref[...] + y


def _attn_out(x2, os, lses, w_o):
    t, d = x2.shape
    tm = min(ROW_TILE, t)
    hw = w_o.shape[0]
    row = lambda w: pl.BlockSpec((tm, w), lambda i: (i, 0))
    return pl.pallas_call(
        _attn_out_body,
        out_shape=jax.ShapeDtypeStruct((t, d), F32),
        grid=(t // tm,),
        in_specs=[row(d), row(hw), row(hw), row(hw), row(LANES), row(LANES), row(LANES),
                  _resident((hw, d))],
        out_specs=row(d),
        compiler_params=_cparams(("parallel",)),
        name="attn_out",
    )(x2, *os, *lses, w_o)


def _proj_res_body(x_ref, y_ref, w_ref, out_ref):
    out_ref[...] = x_ref[...] + jnp.dot(y_ref[...], w_ref[...], preferred_element_type=F32)


def _proj_res(x2, y, w):
    t, d = x2.shape
    tm = min(ROW_TILE, t)
    k = w.shape[0]
    return pl.pallas_call(
        _proj_res_body,
        out_shape=jax.ShapeDtypeStruct((t, d), F32),
        grid=(t // tm,),
        in_specs=[pl.BlockSpec((tm, d), lambda i: (i, 0)), pl.BlockSpec((tm, k), lambda i: (i, 0)),
                  _resident((k, d))],
        out_specs=pl.BlockSpec((tm, d), lambda i: (i, 0)),
        compiler_params=_cparams(("parallel",)),
        name="proj_res",
    )(x2, y, w)


def _ffn_body(x_ref, nw_ref, wup_ref, cp_ref, wd_ref, fw_ref, out_ref, tails, ubuf, act_ref,
              *, final_norm):
    d_ff = wd_ref.shape[0]
    n_chunks = d_ff // FFN_CHUNK
    n_slots = ubuf.shape[0]
    x = x_ref[0]
    hb = _rms(x, nw_ref[...]).astype(BF16)
    tm = x.shape[0]

    @pl.when(pl.program_id(1) == 0)
    def _():
        tails[...] = jnp.zeros_like(tails)

    for c in range(n_chunks):
        halves = []
        for part in range(2):
            idx = part * n_chunks + c
            col = slice(idx * FFN_CHUNK, (idx + 1) * FFN_CHUNK)
            u = jnp.dot(hb, wup_ref[:, col], preferred_element_type=F32)
            halves.append(_conv_taps(u, ubuf.at[idx % n_slots], tails[idx], cp_ref[:, col], 3))
            tails[idx] = u[tm - 8:tm]
        act_ref[:, c * FFN_CHUNK:(c + 1) * FFN_CHUNK] = (_silu(halves[0]) * halves[1]).astype(BF16)

    y = x + jnp.dot(act_ref[...], wd_ref[...], preferred_element_type=F32)
    if final_norm:
        y = _rms(y, fw_ref[...])
    out_ref[0] = y


def _conv_ffn(x, norm_w, w_up, conv_w, conv_b, w_down, final_w):
    bsz, s, d = x.shape
    d_ff = w_down.shape[0]
    tm = min(ROW_TILE, s)
    n_cols = 2 * d_ff // FFN_CHUNK
    cp = jnp.concatenate([conv_w, conv_b[None, :], jnp.zeros((4, 2 * d_ff), F32)], axis=0)
    final_norm = final_w is not None
    fw = (final_w if final_norm else jnp.ones((d,), F32)).reshape(1, d)
    return pl.pallas_call(
        functools.partial(_ffn_body, final_norm=final_norm),
        out_shape=jax.ShapeDtypeStruct((bsz, s, d), F32),
        grid=(bsz, s // tm),
        in_specs=[
            pl.BlockSpec((1, tm, d), lambda b, i: (b, i, 0)),
            _resident((1, d)),
            _resident((d, 2 * d_ff)),
            _resident((8, 2 * d_ff)),
            _resident((d_ff, d)),
            _resident((1, d)),
        ],
        out_specs=pl.BlockSpec((1, tm, d), lambda b, i: (b, i, 0)),
        scratch_shapes=[pltpu.VMEM((n_cols, 8, FFN_CHUNK), F32),
                        pltpu.VMEM((4, tm + 8, FFN_CHUNK), F32),
                        pltpu.VMEM((tm, d_ff), BF16)],
        compiler_params=_cparams(("parallel", "arbitrary")),
        name="conv_ffn",
    )(x, norm_w.reshape(1, d), w_up.astype(BF16), cp, w_down.astype(BF16), fw)


def _ssm_in_body(x_ref, nw_ref, w_ref, wdt_ref, cp_ref, dtb_ref,
                 z_ref, xbc_ref, dt_ref, tails, ubuf):
    x = x_ref[0]
    hb = _rms(x, nw_ref[...]).astype(BF16)
    tm = x.shape[0]
    di = z_ref.shape[2]
    conv_dim = xbc_ref.shape[2]
    n_slots = ubuf.shape[0]

    @pl.when(pl.program_id(1) == 0)
    def _():
        tails[...] = jnp.zeros_like(tails)

    for c in range(di // PROJ_CHUNK):
        col = slice(c * PROJ_CHUNK, (c + 1) * PROJ_CHUNK)
        z_ref[0, :, col] = jnp.dot(hb, w_ref[:, col], preferred_element_type=F32).astype(BF16)

    for c in range(conv_dim // PROJ_CHUNK):
        col = slice(c * PROJ_CHUNK, (c + 1) * PROJ_CHUNK)
        u = jnp.dot(hb, w_ref[:, di + c * PROJ_CHUNK:di + (c + 1) * PROJ_CHUNK],
                    preferred_element_type=F32)
        y = _conv_taps(u, ubuf.at[c % n_slots], tails[c], cp_ref[:, col], 4)
        tails[c] = u[tm - 8:tm]
        xbc_ref[0, :, col] = _silu(y).astype(BF16)

    v = jnp.dot(hb, wdt_ref[...], preferred_element_type=F32) + dtb_ref[...]
    dt_ref[0] = jnp.maximum(v, 0.0) + jnp.log(1.0 + jnp.exp(-jnp.abs(v)))


def _ssm_in(x, norm_w, w_in, conv_w, conv_b, dt_bias):
    bsz, s, d = x.shape
    conv_dim = conv_w.shape[1]
    tm = min(ROW_TILE, s)
    n_main = SSM_D_INNER + conv_dim
    w_main = w_in[:, :n_main].astype(BF16)
    wdt = jnp.pad(w_in[:, n_main:], ((0, 0), (0, LANES - SSM_HEADS))).astype(BF16)
    cp = jnp.concatenate([conv_w, conv_b[None, :], jnp.zeros((3, conv_dim), F32)], axis=0)
    dtb = jnp.pad(dt_bias, (0, LANES - SSM_HEADS)).reshape(1, LANES)
    row = lambda w: pl.BlockSpec((1, tm, w), lambda b, i: (b, i, 0))
    return pl.pallas_call(
        _ssm_in_body,
        out_shape=(jax.ShapeDtypeStruct((bsz, s, SSM_D_INNER), BF16),
                   jax.ShapeDtypeStruct((bsz, s, conv_dim), BF16),
                   jax.ShapeDtypeStruct((bsz, s, LANES), F32)),
        grid=(bsz, s // tm),
        in_specs=[row(d), _resident((1, d)), _resident((d, n_main)), _resident((d, LANES)),
                  _resident((8, conv_dim)), _resident((1, LANES))],
        out_specs=(row(SSM_D_INNER), row(conv_dim), row(LANES)),
        scratch_shapes=[pltpu.VMEM((conv_dim // PROJ_CHUNK, 8, PROJ_CHUNK), F32),
                        pltpu.VMEM((2, tm + 8, PROJ_CHUNK), F32)],
        compiler_params=_cparams(("parallel", "arbitrary")),
        name="ssm_in",
    )(x, norm_w.reshape(1, d), w_main, wdt, cp, dtb)


def _ssd_body(xs_ref, b_ref, c_ref, dt_ref, z_ref, a_ref, e_ref, dsk_ref, nw_ref, y_ref, state):
    q = SSM_CHUNK
    gw = SSM_GROUP_WIDTH

    @pl.when(pl.program_id(1) == 0)
    def _():
        state[...] = jnp.zeros_like(state)

    dt = dt_ref[0]
    row = lax.broadcasted_iota(jnp.int32, (q, LANES), 0)
    lane = lax.broadcasted_iota(jnp.int32, (q, LANES), 1)
    acs = dt * a_ref[...]
    k = 1
    while k < q:
        acs = acs + jnp.where(row >= k, pltpu.roll(acs, k, 0), 0.0)
        k *= 2
    acs_t = acs.T
    eacs = jnp.exp(acs)
    ddec = dt * jnp.exp(acs[q - 1:q] - acs)

    nh = SSM_HEADS
    packed = jnp.where(lane < nh, dt,
                       jnp.where(lane < 2 * nh, pltpu.roll(eacs, nh, 1),
                                 jnp.where(lane < 3 * nh, pltpu.roll(ddec, 2 * nh, 1), 0.0)))
    hi = packed.astype(BF16)
    lo = (packed - hi.astype(F32)).astype(BF16)
    expanded = jnp.dot(jnp.concatenate([hi, lo], axis=1), e_ref[...], preferred_element_type=F32)
    di = SSM_D_INNER
    dt_x = expanded[:, 0:di]
    eacs_x = expanded[:, di:2 * di]
    ddec_x = expanded[:, 2 * di:3 * di]

    causal = (lax.broadcasted_iota(jnp.int32, (q, q), 0) >= lax.broadcasted_iota(jnp.int32, (q, q), 1))
    for g in range(SSM_GROUPS):
        gs = slice(g * gw, (g + 1) * gw)
        ns = slice(g * SSM_STATE, (g + 1) * SSM_STATE)
        xg = xs_ref[0, :, gs].astype(F32)
        bg = b_ref[0, :, ns]
        cg = c_ref[0, :, ns]
        xdt = (xg * dt_x[:, gs]).astype(BF16)
        cb = lax.dot_general(cg, bg, (((1,), (1,)), ((), ())), preferred_element_type=F32)
        s_prev = state[g]
        y_off = jnp.dot(cg, s_prev.astype(BF16), preferred_element_type=F32) * eacs_x[:, gs]
        ys = []
        for r in range(SSM_HEADS_PER_GROUP):
            h = g * SSM_HEADS_PER_GROUP + r
            seg = acs[:, h:h + 1] - acs_t[h:h + 1, :]
            lmat = jnp.exp(jnp.where(causal, seg, -jnp.inf))
            mh = (cb * lmat).astype(BF16)
            ys.append(jnp.dot(mh, xdt[:, r * SSM_HEAD_DIM:(r + 1) * SSM_HEAD_DIM],
                              preferred_element_type=F32))
        y = jnp.concatenate(ys, axis=1) + y_off
        xdd = (xg * ddec_x[:, gs]).astype(BF16)
        upd = lax.dot_general(bg, xdd, (((0,), (0,)), ((), ())), preferred_element_type=F32)
        state[g] = s_prev * eacs_x[q - 1:q, gs] + upd

        y = y + dsk_ref[:, gs] * xg
        gz = y * _silu(z_ref[0, :, gs].astype(F32))
        gz = gz * lax.rsqrt(jnp.mean(gz * gz, axis=-1, keepdims=True) + NORM_EPS)
        y_ref[0, :, gs] = (gz * nw_ref[:, gs]).astype(BF16)


def _ssd_scan(xbc, dt, z, a_log, d_skip, norm_w):
    bsz, s, conv_dim = xbc.shape
    q = SSM_CHUNK
    di = SSM_D_INNER
    gn = SSM_GROUPS * SSM_STATE
    a = jnp.pad(-jnp.exp(a_log), (0, LANES - SSM_HEADS)).reshape(1, LANES)
    src = jnp.arange(LANES)
    dst = jnp.arange(3 * di)
    e1 = ((src[:, None] // SSM_HEADS == dst[None, :] // di)
          & (src[:, None] % SSM_HEADS == (dst[None, :] % di) // SSM_HEAD_DIM)
          & (src[:, None] < 3 * SSM_HEADS))
    e2 = jnp.concatenate([e1, e1], axis=0).astype(BF16)
    dsk = jnp.repeat(d_skip, SSM_HEAD_DIM).reshape(1, di)
    assert gn * 2 + di == conv_dim
    return pl.pallas_call(
        _ssd_body,
        out_shape=jax.ShapeDtypeStruct((bsz, s, di), BF16),
        grid=(bsz, s // q),
        in_specs=[
            pl.BlockSpec((1, q, di), lambda b, c: (b, c, 0)),
            pl.BlockSpec((1, q, gn), lambda b, c: (b, c, di // gn)),
            pl.BlockSpec((1, q, gn), lambda b, c: (b, c, di // gn + 1)),
            pl.BlockSpec((1, q, LANES), lambda b, c: (b, c, 0)),
            pl.BlockSpec((1, q, di), lambda b, c: (b, c, 0)),
            _resident((1, LANES)),
            _resident((2 * LANES, 3 * di)),
            _resident((1, di)),
            _resident((1, di)),
        ],
        out_specs=pl.BlockSpec((1, q, di), lambda b, c: (b, c, 0)),
        scratch_shapes=[pltpu.VMEM((SSM_GROUPS, SSM_STATE, SSM_GROUP_WIDTH), F32)],
        compiler_params=_cparams(("parallel", "arbitrary")),
        name="ssd_scan",
    )(xbc, xbc, xbc, dt, z, a, e2, dsk, norm_w.reshape(1, di))


def _rope_tables(seq):
    half = ROPE_DIM // 2
    pos = jnp.arange(seq, dtype=F32)
    inv_freq = ROPE_THETA ** (-jnp.arange(0, ROPE_DIM, 2, dtype=F32) / ROPE_DIM)
    ang = pos[:, None] * inv_freq[None, :]
    cos, sin = jnp.cos(ang), jnp.sin(ang)
    ones = jnp.ones((seq, HEAD_DIM - ROPE_DIM), F32)
    zeros = jnp.zeros((seq, HEAD_DIM - ROPE_DIM), F32)
    zh = jnp.zeros((seq, half), F32)
    cos_t = jnp.concatenate([cos, cos, ones], axis=1)
    sa_t = jnp.concatenate([-sin, zh, zeros], axis=1)
    sb_t = jnp.concatenate([zh, sin, zeros], axis=1)
    return cos_t, sa_t, sb_t


def _attention_layer(x, norm_w, w_qkv, w_o, tables):
    bsz, s, d = x.shape
    gwidth = 3 * ATTN_HEADS * HEAD_DIM
    os, lses = [], []
    for g, dil in enumerate(ATTN_DILATIONS):
        w_g = w_qkv[:, g * gwidth:(g + 1) * gwidth].astype(BF16)
        qkv = _qkv_proj(x, norm_w, w_g, *tables, dil)
        o, lse = _window_attention(qkv, dil)
        os.append(o.reshape(bsz * s, -1))
        lses.append(lse.reshape(bsz * s, LANES))
    out = _attn_out(x.reshape(bsz * s, d), os, lses, w_o.astype(BF16))
    return out.reshape(bsz, s, d)


def _ssd_layer(x, norm_w, w_in, conv_w, conv_b, dt_bias, a_log, d_skip, gnorm_w, w_out):
    bsz, s, d = x.shape
    z, xbc, dt = _ssm_in(x, norm_w, w_in, conv_w, conv_b, dt_bias)
    y = _ssd_scan(xbc, dt, z, a_log, d_skip, gnorm_w)
    out = _proj_res(x.reshape(bsz * s, d), y.reshape(bsz * s, -1), w_out.astype(BF16))
    return out.reshape(bsz, s, d)


def kernel(x, mix_norm_w, attn_w_qkv, attn_w_o, ssm_w_in, ssm_conv_w, ssm_conv_b, ssm_dt_bias,
           ssm_a_log, ssm_d, ssm_norm_w, ssm_w_out, ffn_norm_w, ffn_w_up, ffn_conv_w, ffn_conv_b,
           ffn_w_down, final_norm_w):
    depth = mix_norm_w.shape[0]
    tables = _rope_tables(x.shape[1])
    for i in range(depth):
        j = i // 2
        if i % 2 == 0:
            x = _attention_layer(x, mix_norm_w[i], attn_w_qkv[j], attn_w_o[j], tables)
        else:
            x = _ssd_layer(x, mix_norm_w[i], ssm_w_in[j], ssm_conv_w[j], ssm_conv_b[j],
                           ssm_dt_bias[j], ssm_a_log[j], ssm_d[j], ssm_norm_w[j], ssm_w_out[j])
        x = _conv_ffn(x, ffn_norm_w[i], ffn_w_up[i], ffn_conv_w[i], ffn_conv_b[i], ffn_w_down[i],
                      final_norm_w if i == depth - 1 else None)
    return x
```

```python
import functools
import math

import jax
import jax.numpy as jnp
from jax import lax
from jax.experimental import pallas as pl
from jax.experimental.pallas import tpu as pltpu

F32 = jnp.float32
BF16 = jnp.bfloat16

NORM_EPS = 1e-5
LOG2_E = math.log2(math.e)
LN_2 = math.log(2.0)
ROPE_THETA = 500000.0

HEAD_DIM = 128
ATTN_HEADS = 8
ATTN_WINDOWS = (128, 512, 2048)
ATTN_DILATIONS = (1, 4, 16)
ATTN_STEPS = 128
ROPE_DIM = HEAD_DIM // 4

SSM_HEAD_DIM = 64
SSM_HEADS = 32
SSM_STATE = 128
SSM_GROUPS = 8
SSM_CHUNK = 128
SSM_D_INNER = SSM_HEADS * SSM_HEAD_DIM
SSM_GROUP_WIDTH = SSM_D_INNER // SSM_GROUPS
SSM_HEADS_PER_GROUP = SSM_HEADS // SSM_GROUPS

LANES = 128
VMEM_LIMIT_BYTES = 56 * 1024 * 1024

ROW_TILE = 512
FFN_CHUNK = 256
SSD_STEP_CHUNKS = 2
PROJ_CHUNK = 512


def _cparams(semantics):
    return pltpu.CompilerParams(dimension_semantics=semantics, vmem_limit_bytes=VMEM_LIMIT_BYTES)


def _resident(shape):
    nd = len(shape)
    return pl.BlockSpec(shape, lambda *_: (0,) * nd, pipeline_mode=pl.Buffered(1))


def _rms(x, w):
    ms = jnp.mean(x * x, axis=-1, keepdims=True)
    return x * lax.rsqrt(ms + NORM_EPS) * w


def _silu(x):
    return x / (1.0 + jnp.exp(-x))


def _conv_taps(u, ub, tail8, cw, taps):
    rows = u.shape[0]
    ub[0:8] = tail8
    ub[8:rows + 8] = u
    y = u * cw[taps - 1:taps] + cw[taps:taps + 1]
    for k in range(taps - 1):
        back = taps - 1 - k
        y = y + ub[8 - back:8 - back + rows] * cw[k:k + 1]
    return y


def _qkv_body(x_ref, nw_ref, w_ref, tab_ref, o_ref, hbuf, *, dil):
    tm = x_ref.shape[1]
    sub = tm // dil
    h = _rms(x_ref[0], nw_ref[...])
    if dil == 1:
        hb = h.astype(BF16)
    else:
        n_tiles = h.shape[1] // LANES
        for j in range(n_tiles):
            hbuf[j] = h[:, j * LANES:(j + 1) * LANES]
        hb = jnp.concatenate(
            [jnp.concatenate([hbuf[j, pl.ds(r, sub, stride=dil), :] for j in range(n_tiles)], axis=1)
             for r in range(dil)], axis=0).astype(BF16)
    cos = tab_ref[0]
    sa = tab_ref[1]
    sb = tab_ref[2]
    width = w_ref.shape[1]
    hw = ATTN_HEADS * HEAD_DIM
    scale = HEAD_DIM ** -0.5 * LOG2_E
    for c in range(width // PROJ_CHUNK):
        c0 = c * PROJ_CHUNK
        y = jnp.dot(hb, w_ref[:, c0:c0 + PROJ_CHUNK], preferred_element_type=F32)
        if c0 < 2 * hw:
            parts = []
            for j in range(PROJ_CHUNK // HEAD_DIM):
                t = y[:, j * HEAD_DIM:(j + 1) * HEAD_DIM]
                t = (t * cos + pltpu.roll(t, HEAD_DIM - ROPE_DIM // 2, 1) * sa
                     + pltpu.roll(t, ROPE_DIM // 2, 1) * sb)
                if c0 < hw:
                    t = t * scale
                parts.append(t)
            y = jnp.concatenate(parts, axis=1)
        yb = y.astype(BF16)
        for r in range(dil):
            o_ref[0, r, :, c0:c0 + PROJ_CHUNK] = yb[r * sub:(r + 1) * sub]


def _qkv_proj(x, norm_w, w_g, tables, dil):
    bsz, s, d = x.shape
    length = s // dil
    tm = min(ROW_TILE, s)
    sub = tm // dil
    width = w_g.shape[1]
    tabs = tables.reshape(3, s // tm, sub, dil, LANES).transpose(0, 1, 3, 2, 4).reshape(3, s, LANES)
    return pl.pallas_call(
        functools.partial(_qkv_body, dil=dil),
        out_shape=jax.ShapeDtypeStruct((bsz, dil, length, width), BF16),
        grid=(bsz, s // tm),
        in_specs=[
            pl.BlockSpec((1, tm, d), lambda b, i: (b, i, 0)),
            _resident((1, d)),
            _resident((d, width)),
            pl.BlockSpec((3, tm, LANES), lambda b, i: (0, i, 0)),
        ],
        out_specs=pl.BlockSpec((1, dil, sub, width), lambda b, i: (b, 0, i, 0)),
        scratch_shapes=[pltpu.VMEM((d // LANES, tm, LANES), F32)],
        compiler_params=_cparams(("parallel", "parallel")),
        name=f"qkv_proj_d{dil}",
    )(x, norm_w.reshape(1, d), w_g, tabs)


def _attn_body(q_ref, kc_ref, vc_ref, kp_ref, vp_ref, o_ref, lse_ref):
    tq = q_ref.shape[2]
    blk = ATTN_STEPS
    first = pl.program_id(2) == 0
    row = lax.broadcasted_iota(jnp.int32, (blk, 2 * blk), 0)
    col = lax.broadcasted_iota(jnp.int32, (blk, 2 * blk), 1)
    band = (col >= row) & (col <= row + blk)
    lane = lax.broadcasted_iota(jnp.int32, (blk, LANES), 1)
    for jb in range(tq // blk):
        r0 = jb * blk
        if jb == 0:
            mask = band & (col >= jnp.where(first, blk, 0))
        else:
            mask = band
        lse_all = jnp.zeros((blk, LANES), F32)
        for h in range(ATTN_HEADS):
            cs = slice(h * HEAD_DIM, (h + 1) * HEAD_DIM)
            q = q_ref[0, 0, r0:r0 + blk, cs]
            if jb == 0:
                k = jnp.concatenate([kp_ref[0, 0, :, cs], kc_ref[0, 0, 0:blk, cs]], axis=0)
                v = jnp.concatenate([vp_ref[0, 0, :, cs], vc_ref[0, 0, 0:blk, cs]], axis=0)
            else:
                k = kc_ref[0, 0, r0 - blk:r0 + blk, cs]
                v = vc_ref[0, 0, r0 - blk:r0 + blk, cs]
            s = lax.dot_general(q, k, (((1,), (1,)), ((), ())), preferred_element_type=F32)
            s = jnp.where(mask, s, -jnp.inf)
            m = jnp.max(s, axis=-1, keepdims=True)
            p = jnp.exp2(s - m)
            den = jnp.sum(p, axis=-1, keepdims=True)
            o = jnp.dot(p.astype(BF16), v, preferred_element_type=F32) / den
            o_ref[0, 0, r0:r0 + blk, cs] = o.astype(BF16)
            lse_all = jnp.where(lane == h, m * LN_2 + jnp.log(den), lse_all)
        lse_ref[0, 0, r0:r0 + blk, :] = lse_all


def _window_attention(qkv, dil):
    bsz, _, length, width = qkv.shape
    hw = width // 3
    tq = min(ROW_TILE, length)
    nprev = tq // ATTN_STEPS
    cur = lambda col: pl.BlockSpec((1, 1, tq, hw), lambda b, r, i: (b, r, i, col))
    prev = lambda col: pl.BlockSpec(
        (1, 1, ATTN_STEPS, hw), lambda b, r, i: (b, r, jnp.maximum(i * nprev - 1, 0), col))
    return pl.pallas_call(
        _attn_body,
        out_shape=(jax.ShapeDtypeStruct((bsz, dil, length, hw), BF16),
                   jax.ShapeDtypeStruct((bsz, dil, length, LANES), F32)),
        grid=(bsz, dil, length // tq),
        in_specs=[cur(0), cur(1), cur(2), prev(1), prev(2)],
        out_specs=(pl.BlockSpec((1, 1, tq, hw), lambda b, r, i: (b, r, i, 0)),
                   pl.BlockSpec((1, 1, tq, LANES), lambda b, r, i: (b, r, i, 0))),
        compiler_params=_cparams(("parallel", "parallel", "parallel")),
        name=f"window_attn_d{dil}",
    )(qkv, qkv, qkv, qkv, qkv)


def _attn_out_body(x_ref, o0_ref, o1_ref, o2_ref, l0_ref, l1_ref, l2_ref, w_ref, out_ref,
                   lbuf, obuf):
    tm = x_ref.shape[1]
    o_refs = (o0_ref, o1_ref, o2_ref)
    l_refs = (l0_ref, l1_ref, l2_ref)

    def to_position_order(ref, buf, g, dil, convert):
        if dil == 1:
            return convert(ref[0, 0])
        sub = tm // dil
        n_tiles = ref.shape[3] // LANES
        for r in range(dil):
            v = convert(ref[0, r])
            for j in range(n_tiles):
                buf[g - 1, j, pl.ds(r, sub, stride=dil), :] = v[:, j * LANES:(j + 1) * LANES]
        return jnp.concatenate([buf[g - 1, j] for j in range(n_tiles)], axis=1)

    ls = [to_position_order(l_refs[g], lbuf, g, dil, lambda v: v)
          for g, dil in enumerate(ATTN_DILATIONS)]
    m = jnp.maximum(jnp.maximum(ls[0], ls[1]), ls[2])
    es = [jnp.exp(l - m) for l in ls]
    inv = 1.0 / (es[0] + es[1] + es[2])
    acc = None
    for g, dil in enumerate(ATTN_DILATIONS):
        wg = es[g] * inv
        og = to_position_order(o_refs[g], obuf, g, dil, lambda v: v.astype(F32))
        parts = [og[:, h * HEAD_DIM:(h + 1) * HEAD_DIM] * wg[:, h:h + 1] for h in range(ATTN_HEADS)]
        og = jnp.concatenate(parts, axis=1)
        acc = og if acc is None else acc + og
    y = jnp.dot(acc.astype(BF16), w_ref[...], preferred_element_type=F32)
    out_ref[0] = x_ref[0] + y


def _attn_out(x, os, lses, w_o):
    bsz, s, d = x.shape
    tm = min(ROW_TILE, s)
    hw = w_o.shape[0]
    grouped = lambda dil, w: pl.BlockSpec((1, dil, tm // dil, w), lambda b, i: (b, 0, i, 0))
    row = pl.BlockSpec((1, tm, d), lambda b, i: (b, i, 0))
    n_strided = len(ATTN_DILATIONS) - 1
    return pl.pallas_call(
        _attn_out_body,
        out_shape=jax.ShapeDtypeStruct((bsz, s, d), F32),
        grid=(bsz, s // tm),
        in_specs=[row] + [grouped(dil, hw) for dil in ATTN_DILATIONS]
        + [grouped(dil, LANES) for dil in ATTN_DILATIONS] + [_resident((hw, d))],
        out_specs=row,
        scratch_shapes=[pltpu.VMEM((n_strided, 1, tm, LANES), F32),
                        pltpu.VMEM((n_strided, hw // LANES, tm, LANES), F32)],
        compiler_params=_cparams(("parallel", "parallel")),
        name="attn_out",
    )(x, *os, *lses, w_o)


def _proj_res_body(x_ref, y_ref, w_ref, out_ref):
    out_ref[...] = x_ref[...] + jnp.dot(y_ref[...], w_ref[...], preferred_element_type=F32)


def _proj_res(x2, y, w):
    t, d = x2.shape
    tm = min(ROW_TILE, t)
    k = w.shape[0]
    return pl.pallas_call(
        _proj_res_body,
        out_shape=jax.ShapeDtypeStruct((t, d), F32),
        grid=(t // tm,),
        in_specs=[pl.BlockSpec((tm, d), lambda i: (i, 0)), pl.BlockSpec((tm, k), lambda i: (i, 0)),
                  _resident((k, d))],
        out_specs=pl.BlockSpec((tm, d), lambda i: (i, 0)),
        compiler_params=_cparams(("parallel",)),
        name="proj_res",
    )(x2, y, w)


def _ffn_body(x_ref, nw_ref, wup_ref, cp_ref, wd_ref, fw_ref, out_ref, tails, ubuf, act_ref,
              *, final_norm):
    d_ff = wd_ref.shape[0]
    n_chunks = d_ff // FFN_CHUNK
    n_slots = ubuf.shape[0]
    x = x_ref[0]
    hb = _rms(x, nw_ref[...]).astype(BF16)
    tm = x.shape[0]

    @pl.when(pl.program_id(1) == 0)
    def _():
        tails[...] = jnp.zeros_like(tails)

    for c in range(n_chunks):
        halves = []
        for part in range(2):
            idx = part * n_chunks + c
            col = slice(idx * FFN_CHUNK, (idx + 1) * FFN_CHUNK)
            u = jnp.dot(hb, wup_ref[:, col], preferred_element_type=F32)
            halves.append(_conv_taps(u, ubuf.at[idx % n_slots], tails[idx], cp_ref[:, col], 3))
            tails[idx] = u[tm - 8:tm]
        act_ref[:, c * FFN_CHUNK:(c + 1) * FFN_CHUNK] = (_silu(halves[0]) * halves[1]).astype(BF16)

    y = x + jnp.dot(act_ref[...], wd_ref[...], preferred_element_type=F32)
    if final_norm:
        y = _rms(y, fw_ref[...])
    out_ref[0] = y


def _conv_ffn(x, norm_w, w_up, conv_w, conv_b, w_down, final_w):
    bsz, s, d = x.shape
    d_ff = w_down.shape[0]
    tm = min(ROW_TILE, s)
    n_cols = 2 * d_ff // FFN_CHUNK
    cp = jnp.concatenate([conv_w, conv_b[None, :], jnp.zeros((4, 2 * d_ff), F32)], axis=0)
    final_norm = final_w is not None
    fw = (final_w if final_norm else jnp.ones((d,), F32)).reshape(1, d)
    return pl.pallas_call(
        functools.partial(_ffn_body, final_norm=final_norm),
        out_shape=jax.ShapeDtypeStruct((bsz, s, d), F32),
        grid=(bsz, s // tm),
        in_specs=[
            pl.BlockSpec((1, tm, d), lambda b, i: (b, i, 0)),
            _resident((1, d)),
            _resident((d, 2 * d_ff)),
            _resident((8, 2 * d_ff)),
            _resident((d_ff, d)),
            _resident((1, d)),
        ],
        out_specs=pl.BlockSpec((1, tm, d), lambda b, i: (b, i, 0)),
        scratch_shapes=[pltpu.VMEM((n_cols, 8, FFN_CHUNK), F32),
                        pltpu.VMEM((4, tm + 8, FFN_CHUNK), F32),
                        pltpu.VMEM((tm, d_ff), BF16)],
        compiler_params=_cparams(("parallel", "arbitrary")),
        name="conv_ffn",
    )(x, norm_w.reshape(1, d), w_up.astype(BF16), cp, w_down.astype(BF16), fw)


def _ssm_in_body(x_ref, nw_ref, w_ref, wdt_ref, cp_ref, dtb_ref,
                 z_ref, xbc_ref, dt_ref, tails, ubuf):
    x = x_ref[0]
    hb = _rms(x, nw_ref[...]).astype(BF16)
    tm = x.shape[0]
    di = z_ref.shape[2]
    conv_dim = xbc_ref.shape[2]
    n_slots = ubuf.shape[0]

    @pl.when(pl.program_id(1) == 0)
    def _():
        tails[...] = jnp.zeros_like(tails)

    for c in range(di // PROJ_CHUNK):
        col = slice(c * PROJ_CHUNK, (c + 1) * PROJ_CHUNK)
        z_ref[0, :, col] = jnp.dot(hb, w_ref[:, col], preferred_element_type=F32).astype(BF16)

    for c in range(conv_dim // PROJ_CHUNK):
        col = slice(c * PROJ_CHUNK, (c + 1) * PROJ_CHUNK)
        u = jnp.dot(hb, w_ref[:, di + c * PROJ_CHUNK:di + (c + 1) * PROJ_CHUNK],
                    preferred_element_type=F32)
        y = _conv_taps(u, ubuf.at[c % n_slots], tails[c], cp_ref[:, col], 4)
        tails[c] = u[tm - 8:tm]
        xbc_ref[0, :, col] = _silu(y).astype(BF16)

    v = jnp.dot(hb, wdt_ref[...], preferred_element_type=F32) + dtb_ref[...]
    dt_ref[0] = jnp.maximum(v, 0.0) + jnp.log(1.0 + jnp.exp(-jnp.abs(v)))


def _ssm_in(x, norm_w, w_in, conv_w, conv_b, dt_bias):
    bsz, s, d = x.shape
    conv_dim = conv_w.shape[1]
    tm = min(ROW_TILE, s)
    n_main = SSM_D_INNER + conv_dim
    w_main = w_in[:, :n_main].astype(BF16)
    wdt = jnp.pad(w_in[:, n_main:], ((0, 0), (0, LANES - SSM_HEADS))).astype(BF16)
    cp = jnp.concatenate([conv_w, conv_b[None, :], jnp.zeros((3, conv_dim), F32)], axis=0)
    dtb = jnp.pad(dt_bias, (0, LANES - SSM_HEADS)).reshape(1, LANES)
    row = lambda w: pl.BlockSpec((1, tm, w), lambda b, i: (b, i, 0))
    return pl.pallas_call(
        _ssm_in_body,
        out_shape=(jax.ShapeDtypeStruct((bsz, s, SSM_D_INNER), BF16),
                   jax.ShapeDtypeStruct((bsz, s, conv_dim), BF16),
                   jax.ShapeDtypeStruct((bsz, s, LANES), F32)),
        grid=(bsz, s // tm),
        in_specs=[row(d), _resident((1, d)), _resident((d, n_main)), _resident((d, LANES)),
                  _resident((8, conv_dim)), _resident((1, LANES))],
        out_specs=(row(SSM_D_INNER), row(conv_dim), row(LANES)),
        scratch_shapes=[pltpu.VMEM((conv_dim // PROJ_CHUNK, 8, PROJ_CHUNK), F32),
                        pltpu.VMEM((2, tm + 8, PROJ_CHUNK), F32)],
        compiler_params=_cparams(("parallel", "arbitrary")),
        name="ssm_in",
    )(x, norm_w.reshape(1, d), w_main, wdt, cp, dtb)


def _ssd_body(xs_ref, b_ref, c_ref, dt_ref, z_ref, a_ref, e_ref, dsk_ref, nw_ref, y_ref, state):
    q = SSM_CHUNK
    gw = SSM_GROUP_WIDTH
    hd = SSM_HEAD_DIM
    di = SSM_D_INNER
    nh = SSM_HEADS

    @pl.when(pl.program_id(1) == 0)
    def _():
        state[...] = jnp.zeros_like(state)

    row = lax.broadcasted_iota(jnp.int32, (q, LANES), 0)
    lane = lax.broadcasted_iota(jnp.int32, (q, LANES), 1)
    causal = (lax.broadcasted_iota(jnp.int32, (q, q), 0) >= lax.broadcasted_iota(jnp.int32, (q, q), 1))
    lane_b = lax.broadcasted_iota(jnp.int32, (q, 2 * hd), 1)

    for ci in range(xs_ref.shape[1] // q):
        rows = slice(ci * q, (ci + 1) * q)
        dt = dt_ref[0, rows, :]
        acs = dt * a_ref[...]
        k = 1
        while k < q:
            acs = acs + jnp.where(row >= k, pltpu.roll(acs, k, 0), 0.0)
            k *= 2
        acs_t = acs.T
        eacs = jnp.exp(acs)
        ddec = dt * jnp.exp(acs[q - 1:q] - acs)

        packed = jnp.where(lane < nh, dt,
                           jnp.where(lane < 2 * nh, pltpu.roll(eacs, nh, 1),
                                     jnp.where(lane < 3 * nh, pltpu.roll(ddec, 2 * nh, 1), 0.0)))
        hi = packed.astype(BF16)
        lo = (packed - hi.astype(F32)).astype(BF16)
        expanded = jnp.dot(jnp.concatenate([hi, lo], axis=1), e_ref[...], preferred_element_type=F32)
        dt_x = expanded[:, 0:di]
        eacs_x = expanded[:, di:2 * di]
        ddec_x = expanded[:, 2 * di:3 * di]

        for g in range(SSM_GROUPS):
            gs = slice(g * gw, (g + 1) * gw)
            ns = slice(g * SSM_STATE, (g + 1) * SSM_STATE)
            xg = xs_ref[0, rows, gs].astype(F32)
            bg = b_ref[0, rows, ns]
            cg = c_ref[0, rows, ns]
            xdt = (xg * dt_x[:, gs]).astype(BF16)
            cb = lax.dot_general(cg, bg, (((1,), (1,)), ((), ())), preferred_element_type=F32)
            s_prev = state[g]
            y_off = jnp.dot(cg, s_prev.astype(BF16), preferred_element_type=F32) * eacs_x[:, gs]
            ys = []
            for pair in range(SSM_HEADS_PER_GROUP // 2):
                ms = []
                for r in (2 * pair, 2 * pair + 1):
                    h = g * SSM_HEADS_PER_GROUP + r
                    seg = acs[:, h:h + 1] - acs_t[h:h + 1, :]
                    lmat = jnp.exp(jnp.where(causal, seg, -jnp.inf))
                    ms.append((cb * lmat).astype(BF16))
                x2 = xdt[:, pair * 2 * hd:(pair + 1) * 2 * hd]
                zero = jnp.zeros_like(x2)
                rhs = jnp.concatenate([jnp.where(lane_b < hd, x2, zero),
                                       jnp.where(lane_b >= hd, x2, zero)], axis=0)
                ys.append(jnp.dot(jnp.concatenate(ms, axis=1), rhs, preferred_element_type=F32))
            y = jnp.concatenate(ys, axis=1) + y_off
            xdd = (xg * ddec_x[:, gs]).astype(BF16)
            upd = lax.dot_general(bg, xdd, (((0,), (0,)), ((), ())), preferred_element_type=F32)
            state[g] = s_prev * eacs_x[q - 1:q, gs] + upd

            y = y + dsk_ref[:, gs] * xg
            gz = y * _silu(z_ref[0, rows, gs].astype(F32))
            gz = gz * lax.rsqrt(jnp.mean(gz * gz, axis=-1, keepdims=True) + NORM_EPS)
            y_ref[0, rows, gs] = (gz * nw_ref[:, gs]).astype(BF16)


def _ssd_scan(xbc, dt, z, a_log, d_skip, norm_w):
    bsz, s, conv_dim = xbc.shape
    q = min(SSD_STEP_CHUNKS * SSM_CHUNK, s)
    di = SSM_D_INNER
    gn = SSM_GROUPS * SSM_STATE
    a = jnp.pad(-jnp.exp(a_log), (0, LANES - SSM_HEADS)).reshape(1, LANES)
    src = jnp.arange(LANES)
    dst = jnp.arange(3 * di)
    e1 = ((src[:, None] // SSM_HEADS == dst[None, :] // di)
          & (src[:, None] % SSM_HEADS == (dst[None, :] % di) // SSM_HEAD_DIM)
          & (src[:, None] < 3 * SSM_HEADS))
    e2 = jnp.concatenate([e1, e1], axis=0).astype(BF16)
    dsk = jnp.repeat(d_skip, SSM_HEAD_DIM).reshape(1, di)
    assert gn * 2 + di == conv_dim
    return pl.pallas_call(
        _ssd_body,
        out_shape=jax.ShapeDtypeStruct((bsz, s, di), BF16),
        grid=(bsz, s // q),
        in_specs=[
            pl.BlockSpec((1, q, di), lambda b, c: (b, c, 0)),
            pl.BlockSpec((1, q, gn), lambda b, c: (b, c, di // gn)),
            pl.BlockSpec((1, q, gn), lambda b, c: (b, c, di // gn + 1)),
            pl.BlockSpec((1, q, LANES), lambda b, c: (b, c, 0)),
            pl.BlockSpec((1, q, di), lambda b, c: (b, c, 0)),
            _resident((1, LANES)),
            _resident((2 * LANES, 3 * di)),
            _resident((1, di)),
            _resident((1, di)),
        ],
        out_specs=pl.BlockSpec((1, q, di), lambda b, c: (b, c, 0)),
        scratch_shapes=[pltpu.VMEM((SSM_GROUPS, SSM_STATE, SSM_GROUP_WIDTH), F32)],
        compiler_params=_cparams(("parallel", "arbitrary")),
        name="ssd_scan",
    )(xbc, xbc, xbc, dt, z, a, e2, dsk, norm_w.reshape(1, di))


def _rope_tables(seq):
    half = ROPE_DIM // 2
    pos = jnp.arange(seq, dtype=F32)
    inv_freq = ROPE_THETA ** (-jnp.arange(0, ROPE_DIM, 2, dtype=F32) / ROPE_DIM)
    ang = pos[:, None] * inv_freq[None, :]
    cos, sin = jnp.cos(ang), jnp.sin(ang)
    ones = jnp.ones((seq, HEAD_DIM - ROPE_DIM), F32)
    zeros = jnp.zeros((seq, HEAD_DIM - ROPE_DIM), F32)
    zh = jnp.zeros((seq, half), F32)
    cos_t = jnp.concatenate([cos, cos, ones], axis=1)
    sa_t = jnp.concatenate([-sin, zh, zeros], axis=1)
    sb_t = jnp.concatenate([zh, sin, zeros], axis=1)
    return jnp.stack([cos_t, sa_t, sb_t])


def _attention_layer(x, norm_w, w_qkv, w_o, tables):
    gwidth = 3 * ATTN_HEADS * HEAD_DIM
    os, lses = [], []
    for g, dil in enumerate(ATTN_DILATIONS):
        w_g = w_qkv[:, g * gwidth:(g + 1) * gwidth].astype(BF16)
        qkv = _qkv_proj(x, norm_w, w_g, tables, dil)
        o, lse = _window_attention(qkv, dil)
        os.append(o)
        lses.append(lse)
    return _attn_out(x, os, lses, w_o.astype(BF16))


def _ssd_layer(x, norm_w, w_in, conv_w, conv_b, dt_bias, a_log, d_skip, gnorm_w, w_out):
    bsz, s, d = x.shape
    z, xbc, dt = _ssm_in(x, norm_w, w_in, conv_w, conv_b, dt_bias)
    y = _ssd_scan(xbc, dt, z, a_log, d_skip, gnorm_w)
    out = _proj_res(x.reshape(bsz * s, d), y.reshape(bsz * s, -1), w_out.astype(BF16))
    return out.reshape(bsz, s, d)


def kernel(x, mix_norm_w, attn_w_qkv, attn_w_o, ssm_w_in, ssm_conv_w, ssm_conv_b, ssm_dt_bias,
           ssm_a_log, ssm_d, ssm_norm_w, ssm_w_out, ffn_norm_w, ffn_w_up, ffn_conv_w, ffn_conv_b,
           ffn_w_down, final_norm_w):
    depth = mix_norm_w.shape[0]
    tables = _rope_tables(x.shape[1])
    for i in range(depth):
        j = i // 2
        if i % 2 == 0:
            x = _attention_layer(x, mix_norm_w[i], attn_w_qkv[j], attn_w_o[j], tables)
        else:
            x = _ssd_layer(x, mix_norm_w[i], ssm_w_in[j], ssm_conv_w[j], ssm_conv_b[j],
                           ssm_dt_bias[j], ssm_a_log[j], ssm_d[j], ssm_norm_w[j], ssm_w_out[j])
        x = _conv_ffn(x, ffn_norm_w[i], ffn_w_up[i], ffn_conv_w[i], ffn_conv_b[i], ffn_w_down[i],
                      final_norm_w if i == depth - 1 else None)
    return x
```
